```python
import math
import jax, jax.numpy as jnp
from jax import lax
import numpy as np

D_MODEL = 1024
BATCH = 8
SEQ = 8192
DEPTH = 4
DEC_BATCH = 32
DEC_SEQ = 16
PAST_LEN = 1024

CHUNK = 64
N_META = 16
N_EVEN = (DEPTH + 1) // 2
N_ODD = DEPTH // 2
D_A = D_MODEL // 2
D_B = D_MODEL // 2
W_A = 31
W_B = 3
W_C = 4
D_RNN = D_MODEL
LRU_HEADS = 8
LRU_BW = D_RNN // LRU_HEADS
LRU_C = 8.0
D_FF = 4 * D_MODEL
P_EVEN = 2 * D_A + 3 * D_B
P_ODD = 2 * D_RNN
ALPHA = (2 * DEPTH) ** 0.25
BETA = (8 * DEPTH) ** -0.25
LN_EPS = 1e-5

kernel_name = 'hybrid_streaming_conv_rglru_encoder_step'


def layer_norm(x, g, b):
    xf = x.astype(jnp.float32)
    mu = jnp.mean(xf, axis=-1, keepdims=True)
    var = jnp.mean(jnp.square(xf - mu), axis=-1, keepdims=True)
    y = (xf - mu) * lax.rsqrt(var + LN_EPS)
    return (y * g.astype(jnp.float32) + b.astype(jnp.float32)).astype(x.dtype)


def causal_dwconv(x, buf, w):
    xp = jnp.concatenate([buf.astype(x.dtype), x], axis=1)
    out = lax.conv_general_dilated(
        xp, w[:, None, :].astype(x.dtype), window_strides=(1,), padding='VALID',
        dimension_numbers=('NWC', 'WIO', 'NWC'), feature_group_count=x.shape[-1])
    new_buf = xp[:, xp.shape[1] - (w.shape[0] - 1):]
    return out, new_buf


def linear_recurrence(a, b, h0):
    b = b.at[:, 0].add(a[:, 0] * h0)
    def combine(l, r):
        return (l[0] * r[0], r[0] * l[1] + r[1])
    _, h = lax.associative_scan(combine, (a, b), axis=1)
    return h


def conv_pair_mixer(x, buf_a, buf_b, w_in, b_in, conv_a_w, conv_a_b, ln_a_g, ln_a_b, conv_b_w, w_out, b_out):
    p = jnp.einsum('btd,dp->btp', x, w_in) + b_in
    a_val = p[..., :D_A]
    a_gate = p[..., D_A:2 * D_A]
    g_b = p[..., 2 * D_A:2 * D_A + D_B]
    g_c = p[..., 2 * D_A + D_B:2 * D_A + 2 * D_B]
    h_b = p[..., 2 * D_A + 2 * D_B:]
    u = a_val * jax.nn.sigmoid(a_gate)
    ca, new_buf_a = causal_dwconv(u, buf_a, conv_a_w)
    y_a = jax.nn.silu(layer_norm(ca + conv_a_b, ln_a_g, ln_a_b))
    v = g_c * h_b
    cb, new_buf_b = causal_dwconv(v, buf_b, conv_b_w)
    y_b = g_b * cb
    y = jnp.einsum('btc,cd->btd', jnp.concatenate([y_a, y_b], axis=-1), w_out) + b_out
    return y, new_buf_a, new_buf_b


def rglru_mixer(x, buf_c, h0, w_in, b_in, conv_c_w, conv_c_b, w_gate_a, b_gate_a, w_gate_x, b_gate_x, lru_lambda, w_out, b_out):
    bsz, t = x.shape[0], x.shape[1]
    p = jnp.einsum('btd,dp->btp', x, w_in) + b_in
    gate = p[..., :D_RNN]
    u = p[..., D_RNN:]
    xc, new_buf_c = causal_dwconv(u, buf_c, conv_c_w)
    xc = xc + conv_c_b
    xh = xc.reshape(bsz, t, LRU_HEADS, LRU_BW)
    r = jax.nn.sigmoid(jnp.einsum('bthi,hij->bthj', xh, w_gate_a).reshape(bsz, t, D_RNN) + b_gate_a)
    i = jax.nn.sigmoid(jnp.einsum('bthi,hij->bthj', xh, w_gate_x).reshape(bsz, t, D_RNN) + b_gate_x)
    log_a = -LRU_C * r.astype(jnp.float32) * jax.nn.softplus(-lru_lambda.astype(jnp.float32))
    a = jnp.exp(log_a)
    mult = jnp.sqrt(jnp.maximum(-jnp.expm1(2.0 * log_a), 0.0))
    bterm = mult * (i.astype(jnp.float32) * xc.astype(jnp.float32))
    h = linear_recurrence(a, bterm, h0.astype(jnp.float32))
    y = h.astype(x.dtype) * jax.nn.gelu(gate)
    out = jnp.einsum('btr,rd->btd', y, w_out) + b_out
    return out, new_buf_c, h[:, -1]


def squared_relu_mlp(x, w1, w2):
    hid = jnp.square(jax.nn.relu(jnp.einsum('btd,df->btf', x, w1)))
    return jnp.einsum('btf,fd->btd', hid, w2)


def trunk(x, buf_a, buf_b, buf_c, h_lru, prm):
    new_a, new_b, new_c, new_h = [], [], [], []
    for l in range(DEPTH):
        if l % 2 == 0:
            e = l // 2
            m, na, nb = conv_pair_mixer(
                x, buf_a[e], buf_b[e], prm['w_in_e'][e], prm['b_in_e'][e], prm['conv_a_w'][e], prm['conv_a_b'][e],
                prm['ln_a_g'][e], prm['ln_a_b'][e], prm['conv_b_w'][e], prm['w_out_e'][e], prm['b_out_e'][e])
            new_a.append(na)
            new_b.append(nb)
        else:
            o = l // 2
            m, nc, nh = rglru_mixer(
                x, buf_c[o], h_lru[o], prm['w_in_o'][o], prm['b_in_o'][o], prm['conv_c_w'][o], prm['conv_c_b'][o],
                prm['w_gate_a'][o], prm['b_gate_a'][o], prm['w_gate_x'][o], prm['b_gate_x'][o], prm['lru_lambda'][o],
                prm['w_out_o'][o], prm['b_out_o'][o])
            new_c.append(nc)
            new_h.append(nh)
        x = layer_norm(ALPHA * x + m, prm['ln1_g'][l], prm['ln1_b'][l])
        x = layer_norm(ALPHA * x + squared_relu_mlp(x, prm['w_mlp1'][l], prm['w_mlp2'][l]), prm['ln2_g'][l], prm['ln2_b'][l])
    return x, jnp.stack(new_a), jnp.stack(new_b), jnp.stack(new_c), jnp.stack(new_h)


def setup_inputs(seed: int = 0) -> dict:
    key = jax.random.key(seed)
    ks = iter(jax.random.split(key, 40))
    f32 = jnp.float32
    def nrm(shape, scale):
        return jax.random.normal(next(ks), shape, f32) * scale
    u = jax.random.uniform(next(ks), (N_ODD, D_RNN), f32, 0.9, 0.999)
    s = u ** (1.0 / LRU_C)
    lru_lambda = jnp.log(s) - jnp.log1p(-s)
    return {
        'x_prompt': nrm((BATCH, SEQ, D_MODEL), 1.0),
        'x_sample': nrm((DEC_BATCH, DEC_SEQ, D_MODEL), 1.0),
        'state_conv_a': nrm((N_EVEN, DEC_BATCH, W_A - 1, D_A), 1.0),
        'state_conv_b': nrm((N_EVEN, DEC_BATCH, W_B - 1, D_B), 1.0),
        'state_conv_c': nrm((N_ODD, DEC_BATCH, W_C - 1, D_RNN), 1.0),
        'state_lru': nrm((N_ODD, DEC_BATCH, D_RNN), 0.5),
        'meta_tokens': nrm((N_META, D_MODEL), 1.0),
        'ln1_g': 1.0 + nrm((DEPTH, D_MODEL), 0.02),
        'ln1_b': nrm((DEPTH, D_MODEL), 0.02),
        'ln2_g': 1.0 + nrm((DEPTH, D_MODEL), 0.02),
        'ln2_b': nrm((DEPTH, D_MODEL), 0.02),
        'w_in_e': nrm((N_EVEN, D_MODEL, P_EVEN), D_MODEL ** -0.5),
        'b_in_e': nrm((N_EVEN, P_EVEN), 0.02),
        'conv_a_w': nrm((N_EVEN, W_A, D_A), W_A ** -0.5),
        'conv_a_b': nrm((N_EVEN, D_A), 0.02),
        'ln_a_g': 1.0 + nrm((N_EVEN, D_A), 0.02),
        'ln_a_b': nrm((N_EVEN, D_A), 0.02),
        'conv_b_w': nrm((N_EVEN, W_B, D_B), W_B ** -0.5),
        'w_out_e': nrm((N_EVEN, D_A + D_B, D_MODEL), BETA * (D_A + D_B) ** -0.5),
        'b_out_e': nrm((N_EVEN, D_MODEL), 0.02),
        'w_in_o': nrm((N_ODD, D_MODEL, P_ODD), D_MODEL ** -0.5),
        'b_in_o': nrm((N_ODD, P_ODD), 0.02),
        'conv_c_w': nrm((N_ODD, W_C, D_RNN), W_C ** -0.5),
        'conv_c_b': nrm((N_ODD, D_RNN), 0.02),
        'w_gate_a': nrm((N_ODD, LRU_HEADS, LRU_BW, LRU_BW), LRU_BW ** -0.5),
        'b_gate_a': nrm((N_ODD, D_RNN), 0.02),
        'w_gate_x': nrm((N_ODD, LRU_HEADS, LRU_BW, LRU_BW), LRU_BW ** -0.5),
        'b_gate_x': nrm((N_ODD, D_RNN), 0.02),
        'lru_lambda': lru_lambda,
        'w_out_o': nrm((N_ODD, D_RNN, D_MODEL), BETA * D_RNN ** -0.5),
        'b_out_o': nrm((N_ODD, D_MODEL), 0.02),
        'w_mlp1': nrm((DEPTH, D_MODEL, D_FF), D_MODEL ** -0.5),
        'w_mlp2': nrm((DEPTH, D_FF, D_MODEL), BETA * D_FF ** -0.5),
    }


def reference(x_prompt, x_sample, state_conv_a, state_conv_b, state_conv_c, state_lru, meta_tokens,
              ln1_g, ln1_b, ln2_g, ln2_b,
              w_in_e, b_in_e, conv_a_w, conv_a_b, ln_a_g, ln_a_b, conv_b_w, w_out_e, b_out_e,
              w_in_o, b_in_o, conv_c_w, conv_c_b, w_gate_a, b_gate_a, w_gate_x, b_gate_x, lru_lambda, w_out_o, b_out_o,
              w_mlp1, w_mlp2):
    prm = dict(ln1_g=ln1_g, ln1_b=ln1_b, ln2_g=ln2_g, ln2_b=ln2_b,
               w_in_e=w_in_e, b_in_e=b_in_e, conv_a_w=conv_a_w, conv_a_b=conv_a_b, ln_a_g=ln_a_g, ln_a_b=ln_a_b,
               conv_b_w=conv_b_w, w_out_e=w_out_e, b_out_e=b_out_e,
               w_in_o=w_in_o, b_in_o=b_in_o, conv_c_w=conv_c_w, conv_c_b=conv_c_b, w_gate_a=w_gate_a,
               b_gate_a=b_gate_a, w_gate_x=w_gate_x, b_gate_x=b_gate_x, lru_lambda=lru_lambda,
               w_out_o=w_out_o, b_out_o=b_out_o, w_mlp1=w_mlp1, w_mlp2=w_mlp2)
    bsz = x_prompt.shape[0]
    dt = x_prompt.dtype
    meta = jnp.broadcast_to(meta_tokens.astype(dt)[None], (bsz, N_META, D_MODEL))
    xp = jnp.concatenate([meta, x_prompt], axis=1)
    zero_a = jnp.zeros((N_EVEN, bsz, W_A - 1, D_A), dt)
    zero_b = jnp.zeros((N_EVEN, bsz, W_B - 1, D_B), dt)
    zero_c = jnp.zeros((N_ODD, bsz, W_C - 1, D_RNN), dt)
    zero_h = jnp.zeros((N_ODD, bsz, D_RNN), jnp.float32)
    yp_full, sa_p, sb_p, sc_p, sh_p = trunk(xp, zero_a, zero_b, zero_c, zero_h, prm)
    y_prompt = yp_full[:, N_META:]
    y_sample, sa_s, sb_s, sc_s, sh_s = trunk(x_sample, state_conv_a, state_conv_b, state_conv_c, state_lru, prm)
    return (y_prompt, y_sample, sa_p, sb_p, sc_p, sh_p, sa_s, sb_s, sc_s, sh_s)
```

```python
import functools

import jax
import jax.numpy as jnp
from jax import lax
from jax.experimental import pallas as pl
from jax.experimental.pallas import tpu as pltpu

D_MODEL = 1024
DEPTH = 4
D_A = D_MODEL // 2
D_B = D_MODEL // 2
W_A = 31
W_B = 3
W_C = 4
D_RNN = D_MODEL
LRU_HEADS = 8
LRU_BW = D_RNN // LRU_HEADS
LRU_C = 8.0
D_FF = 4 * D_MODEL
ALPHA = (2 * DEPTH) ** 0.25
LN_EPS = 1e-5

F32 = jnp.float32
BF16 = jnp.bfloat16

LANES = 128
SUBLANES = 8
BF16_ROWS = 16
VMEM_LIMIT_BYTES = 56 * 1024 * 1024
ROW_TILE_TARGET = 512
MATMUL_CHUNK_TARGET = 256
CONV_CHUNK = 64


def _largest_divisor(n, target, multiple):
    for d in range(min(n, target), 0, -1):
        if n % d == 0 and d % multiple == 0:
            return d
    raise ValueError(f"no divisor of {n} that is a multiple of {multiple}")


def _layer_norm(x, g, b):
    mu = jnp.mean(x, axis=-1, keepdims=True)
    xc = x - mu
    var = jnp.mean(xc * xc, axis=-1, keepdims=True)
    return xc * lax.rsqrt(var + LN_EPS) * g + b


def _bdot(a, b):
    return jnp.dot(a, b, preferred_element_type=F32)


def _chunk_loop(n_rows, chunk, body):
    def step(i, carry):
        body(pl.multiple_of(i * chunk, chunk))
        return carry
    lax.fori_loop(0, n_rows // chunk, step, 0)


def _even_mixer_kernel(x_ref, sa_ref, sb_ref, w_in_ref, b_in_ref, caw_ref, cab_ref,
                       lag_ref, lab_ref, cbw_ref, w_out_ref, b_out_ref, g1_ref, b1_ref,
                       o_ref, na_ref, nb_ref,
                       u_ext, v_ext, gb_s, ycat, *, B, R, CM, CH, n_tiles):
    HA = (W_A - 1) * B
    HB = (W_B - 1) * B

    @pl.when(pl.program_id(0) == 0)
    def _():
        u_ext[0:HA, :] = sa_ref[...]
        v_ext[0:HB, :] = sb_ref[...]

    def in_proj(r0):
        xb = x_ref[pl.ds(r0, CM), :].astype(BF16)

        def sec(s):
            lo = s * D_A
            return _bdot(xb, w_in_ref[:, lo:lo + D_A]) + b_in_ref[:, lo:lo + D_A]

        u = sec(0) * jax.nn.sigmoid(sec(1))
        u_ext[pl.ds(pl.multiple_of(HA + r0, SUBLANES), CM), :] = u
        gb_s[pl.ds(r0, CM), :] = sec(2)
        v = sec(3) * sec(4)
        v_ext[pl.ds(pl.multiple_of(HB + r0, SUBLANES), CM), :] = v

    _chunk_loop(R, CM, in_proj)

    def convs(r0):
        accs = []
        for j in range(D_A // LANES):
            ls = slice(j * LANES, (j + 1) * LANES)
            acc = jnp.zeros((CH, LANES), F32)
            for k in range(W_A):
                rk = pl.multiple_of(r0 + k * B, SUBLANES)
                acc = acc + caw_ref[k:k + 1, ls] * u_ext[pl.ds(rk, CH), ls]
            accs.append(acc)
        ca = jnp.concatenate(accs, axis=-1) + cab_ref[...]
        y_a = jax.nn.silu(_layer_norm(ca, lag_ref[...], lab_ref[...]))
        ycat[pl.ds(r0, CH), 0:D_A] = y_a.astype(BF16)
        cb = jnp.zeros((CH, D_B), F32)
        for k in range(W_B):
            rk = pl.multiple_of(r0 + k * B, SUBLANES)
            cb = cb + cbw_ref[k:k + 1, :] * v_ext[pl.ds(rk, CH), :]
        y_b = gb_s[pl.ds(r0, CH), :] * cb
        ycat[pl.ds(r0, CH), D_A:D_A + D_B] = y_b.astype(BF16)

    _chunk_loop(R, CH, convs)

    def out_proj(r0):
        m = _bdot(ycat[pl.ds(r0, CM), :], w_out_ref[...]) + b_out_ref[...]
        z = ALPHA * x_ref[pl.ds(r0, CM), :] + m
        o_ref[pl.ds(r0, CM), :] = _layer_norm(z, g1_ref[...], b1_ref[...])

    _chunk_loop(R, CM, out_proj)

    na_ref[...] = u_ext[R:R + HA, :]
    nb_ref[...] = v_ext[R:R + HB, :]
    if n_tiles > 1:
        u_ext[0:HA, :] = u_ext[R:R + HA, :]
        v_ext[0:HB, :] = v_ext[R:R + HB, :]


def _odd_mixer_kernel(x_ref, sc_ref, sh_ref, w_in_ref, b_in_ref, ccw_ref, ccb_ref,
                      wg_ref, bga_ref, bgx_ref, lam_ref, w_out_ref, b_out_ref, g1_ref, b1_ref,
                      o_ref, nc_ref, nh_ref,
                      uc_ext, a_s, b_s, gg_s, h_s, *, B, R, CM, LB, n_tiles):
    HC = (W_C - 1) * B

    @pl.when(pl.program_id(0) == 0)
    def _():
        uc_ext[0:HC, :] = sc_ref[...]
        h_s[...] = sh_ref[...]

    def in_proj(r0):
        xb = x_ref[pl.ds(r0, CM), :].astype(BF16)
        gate = _bdot(xb, w_in_ref[:, 0:D_RNN]) + b_in_ref[:, 0:D_RNN]
        gg_s[pl.ds(r0, CM), :] = jax.nn.gelu(gate)
        u = _bdot(xb, w_in_ref[:, D_RNN:2 * D_RNN]) + b_in_ref[:, D_RNN:2 * D_RNN]
        uc_ext[pl.ds(pl.multiple_of(HC + r0, SUBLANES), CM), :] = u

    _chunk_loop(R, CM, in_proj)

    lam = lam_ref[...]
    sp = jnp.maximum(-lam, 0.0) + jnp.log1p(jnp.exp(-jnp.abs(lam)))

    def gates(r0):
        xc = jnp.zeros((CM, D_RNN), F32)
        for k in range(W_C):
            rk = pl.multiple_of(r0 + k * B, SUBLANES)
            xc = xc + ccw_ref[k:k + 1, :] * uc_ext[pl.ds(rk, CM), :]
        xc = xc + ccb_ref[...]
        for h in range(LRU_HEADS):
            hs = slice(h * LRU_BW, (h + 1) * LRU_BW)
            xh = xc[:, hs]
            g = _bdot(xh.astype(BF16), wg_ref[h])
            r = jax.nn.sigmoid(g[:, 0:LRU_BW] + bga_ref[:, hs])
            i = jax.nn.sigmoid(g[:, LRU_BW:2 * LRU_BW] + bgx_ref[:, hs])
            log_a = -LRU_C * r * sp[:, hs]
            a = jnp.exp(log_a)
            mult = jnp.sqrt(jnp.maximum(1.0 - a * a, 0.0))
            a_s[pl.ds(r0, CM), hs] = a
            b_s[pl.ds(r0, CM), hs] = mult * (i * xh)

    _chunk_loop(R, CM, gates)

    for j in range(D_RNN // LB):
        ls = slice(j * LB, (j + 1) * LB)

        def scan_step(t, h, ls=ls):
            r = pl.multiple_of(t * B, SUBLANES)
            h = a_s[pl.ds(r, B), ls] * h + b_s[pl.ds(r, B), ls]
            b_s[pl.ds(r, B), ls] = h
            return h

        h_s[:, ls] = lax.fori_loop(0, R // B, scan_step, h_s[:, ls], unroll=8)

    def out_proj(r0):
        y = (b_s[pl.ds(r0, CM), :] * gg_s[pl.ds(r0, CM), :]).astype(BF16)
        m = _bdot(y, w_out_ref[...]) + b_out_ref[...]
        z = ALPHA * x_ref[pl.ds(r0, CM), :] + m
        o_ref[pl.ds(r0, CM), :] = _layer_norm(z, g1_ref[...], b1_ref[...])

    _chunk_loop(R, CM, out_proj)

    nc_ref[...] = uc_ext[R:R + HC, :]
    nh_ref[...] = h_s[...]
    if n_tiles > 1:
        uc_ext[0:HC, :] = uc_ext[R:R + HC, :]


def _mlp_kernel(x_ref, w1_ref, w2_ref, g2_ref, b2_ref, o_ref, *, R, CM):
    def chunk(r0):
        x = x_ref[pl.ds(r0, CM), :]
        hid = jnp.square(jnp.maximum(_bdot(x.astype(BF16), w1_ref[...]), 0.0))
        m = _bdot(hid.astype(BF16), w2_ref[...])
        o_ref[pl.ds(r0, CM), :] = _layer_norm(ALPHA * x + m, g2_ref[...], b2_ref[...])

    _chunk_loop(R, CM, chunk)


def _const_spec(shape):
    zeros = (0,) * len(shape)
    return pl.BlockSpec(shape, lambda i: zeros, pipeline_mode=pl.Buffered(1))


def _row_spec(R, width):
    return pl.BlockSpec((R, width), lambda i: (i, 0))


def _tiling(n_rows, B):
    if n_rows <= 2 * ROW_TILE_TARGET:
        R = n_rows
    else:
        R = _largest_divisor(n_rows, ROW_TILE_TARGET, max(B, BF16_ROWS))
    n_tiles = n_rows // R
    CM = _largest_divisor(R, MATMUL_CHUNK_TARGET, BF16_ROWS)
    return R, CM, n_tiles


def _params():
    return pltpu.CompilerParams(dimension_semantics=("arbitrary",),
                                vmem_limit_bytes=VMEM_LIMIT_BYTES)


def _even_mixer(x, sa, sb, prm, B):
    n_rows = x.shape[0]
    R, CM, n_tiles = _tiling(n_rows, B)
    CH = _largest_divisor(R, CONV_CHUNK, BF16_ROWS)
    HA, HB = (W_A - 1) * B, (W_B - 1) * B
    assert n_tiles == 1 or R >= HA
    consts = [sa, sb, prm['w_in'], prm['b_in'], prm['conv_a_w'], prm['conv_a_b'], prm['ln_a_g'],
              prm['ln_a_b'], prm['conv_b_w'], prm['w_out'], prm['b_out'], prm['ln1_g'], prm['ln1_b']]
    kern = functools.partial(_even_mixer_kernel, B=B, R=R, CM=CM, CH=CH, n_tiles=n_tiles)
    return pl.pallas_call(
        kern,
        grid=(n_tiles,),
        in_specs=[_row_spec(R, D_MODEL)] + [_const_spec(c.shape) for c in consts],
        out_specs=[_row_spec(R, D_MODEL), _const_spec((HA, D_A)), _const_spec((HB, D_B))],
        out_shape=[jax.ShapeDtypeStruct((n_rows, D_MODEL), F32),
                   jax.ShapeDtypeStruct((HA, D_A), F32),
                   jax.ShapeDtypeStruct((HB, D_B), F32)],
        scratch_shapes=[pltpu.VMEM((HA + R, D_A), F32), pltpu.VMEM((HB + R, D_B), F32),
                        pltpu.VMEM((R, D_B), F32), pltpu.VMEM((R, D_A + D_B), BF16)],
        compiler_params=_params(),
        name="even_mixer",
    )(x, *consts)


def _odd_mixer(x, sc, sh, prm, B):
    n_rows = x.shape[0]
    R, CM, n_tiles = _tiling(n_rows, B)
    HC = (W_C - 1) * B
    LB = D_RNN if B <= SUBLANES else 2 * LANES
    assert n_tiles == 1 or R >= HC
    consts = [sc, sh, prm['w_in'], prm['b_in'], prm['conv_c_w'], prm['conv_c_b'], prm['w_gate'],
              prm['b_gate_a'], prm['b_gate_x'], prm['lru_lambda'], prm['w_out'], prm['b_out'],
              prm['ln1_g'], prm['ln1_b']]
    kern = functools.partial(_odd_mixer_kernel, B=B, R=R, CM=CM, LB=LB, n_tiles=n_tiles)
    return pl.pallas_call(
        kern,
        grid=(n_tiles,),
        in_specs=[_row_spec(R, D_MODEL)] + [_const_spec(c.shape) for c in consts],
        out_specs=[_row_spec(R, D_MODEL), _const_spec((HC, D_RNN)), _const_spec((B, D_RNN))],
        out_shape=[jax.ShapeDtypeStruct((n_rows, D_MODEL), F32),
                   jax.ShapeDtypeStruct((HC, D_RNN), F32),
                   jax.ShapeDtypeStruct((B, D_RNN), F32)],
        scratch_shapes=[pltpu.VMEM((HC + R, D_RNN), F32), pltpu.VMEM((R, D_RNN), F32),
                        pltpu.VMEM((R, D_RNN), F32), pltpu.VMEM((R, D_RNN), F32),
                        pltpu.VMEM((B, D_RNN), F32)],
        compiler_params=_params(),
        name="odd_mixer",
    )(x, *consts)


def _mlp(x, prm, B):
    n_rows = x.shape[0]
    R, CM, n_tiles = _tiling(n_rows, B)
    consts = [prm['w_mlp1'], prm['w_mlp2'], prm['ln2_g'], prm['ln2_b']]
    kern = functools.partial(_mlp_kernel, R=R, CM=CM)
    return pl.pallas_call(
        kern,
        grid=(n_tiles,),
        in_specs=[_row_spec(R, D_MODEL)] + [_const_spec(c.shape) for c in consts],
        out_specs=_row_spec(R, D_MODEL),
        out_shape=jax.ShapeDtypeStruct((n_rows, D_MODEL), F32),
        compiler_params=_params(),
        name="mlp",
    )(x, *consts)


def _trunk(x, states, layers, B):
    new_states = []
    for l in range(DEPTH):
        prm = layers[l]
        if l % 2 == 0:
            x, na, nb = _even_mixer(x, states[l][0], states[l][1], prm, B)
            new_states.append((na, nb))
        else:
            x, nc, nh = _odd_mixer(x, states[l][0], states[l][1], prm, B)
            new_states.append((nc, nh))
        x = _mlp(x, prm, B)
    return x, new_states


def _to_rows(a, b_pad):
    bsz, t, c = a.shape
    a = jnp.transpose(a, (1, 0, 2))
    if b_pad > bsz:
        a = jnp.pad(a, ((0, 0), (0, b_pad - bsz), (0, 0)))
    return a.reshape(t * b_pad, c)


def _from_rows(rows, b_pad, lo, hi):
    c = rows.shape[-1]
    a = rows.reshape(-1, b_pad, c)[:, lo:hi]
    return jnp.transpose(a, (1, 0, 2))


def kernel(x_prompt, x_sample, state_conv_a, state_conv_b, state_conv_c, state_lru, meta_tokens,
           ln1_g, ln1_b, ln2_g, ln2_b,
           w_in_e, b_in_e, conv_a_w, conv_a_b, ln_a_g, ln_a_b, conv_b_w, w_out_e, b_out_e,
           w_in_o, b_in_o, conv_c_w, conv_c_b, w_gate_a, b_gate_a, w_gate_x, b_gate_x, lru_lambda,
           w_out_o, b_out_o, w_mlp1, w_mlp2):
    bsz, seq, _ = x_prompt.shape
    dec_b, dec_t, _ = x_sample.shape
    n_meta = meta_tokens.shape[0]
    assert dec_t == n_meta, "meta tokens ride along with the sample streams as one more sequence"
    assert bsz % SUBLANES == 0

    row = lambda v: v.reshape(1, -1)
    layers = []
    for l in range(DEPTH):
        i = l // 2
        prm = dict(ln1_g=row(ln1_g[l]), ln1_b=row(ln1_b[l]), ln2_g=row(ln2_g[l]), ln2_b=row(ln2_b[l]),
                   w_mlp1=w_mlp1[l].astype(BF16), w_mlp2=w_mlp2[l].astype(BF16))
        if l % 2 == 0:
            prm.update(w_in=w_in_e[i].astype(BF16), b_in=row(b_in_e[i]), conv_a_w=conv_a_w[i],
                       conv_a_b=row(conv_a_b[i]), ln_a_g=row(ln_a_g[i]), ln_a_b=row(ln_a_b[i]),
                       conv_b_w=conv_b_w[i], w_out=w_out_e[i].astype(BF16), b_out=row(b_out_e[i]))
        else:
            prm.update(w_in=w_in_o[i].astype(BF16), b_in=row(b_in_o[i]), conv_c_w=conv_c_w[i],
                       conv_c_b=row(conv_c_b[i]),
                       w_gate=jnp.concatenate([w_gate_a[i], w_gate_x[i]], axis=-1).astype(BF16),
                       b_gate_a=row(b_gate_a[i]), b_gate_x=row(b_gate_x[i]),
                       lru_lambda=row(lru_lambda[i]), w_out=w_out_o[i].astype(BF16),
                       b_out=row(b_out_o[i]))
        layers.append(prm)

    n_short = dec_b + 1
    bp = -(-n_short // SUBLANES) * SUBLANES
    xs = jnp.concatenate([meta_tokens[None].astype(x_sample.dtype), x_sample], axis=0)
    with_zero = lambda s: jnp.concatenate([jnp.zeros_like(s[:1]), s], axis=0)
    states = []
    for l in range(DEPTH):
        i = l // 2
        if l % 2 == 0:
            states.append((_to_rows(with_zero(state_conv_a[i]), bp), _to_rows(with_zero(state_conv_b[i]), bp)))
        else:
            states.append((_to_rows(with_zero(state_conv_c[i]), bp),
                           _to_rows(with_zero(state_lru[i])[:, None, :], bp)))
    ys_rows, new_short = _trunk(_to_rows(xs, bp), states, layers, bp)

    def meta_state(rows):
        c = rows.shape[-1]
        m = rows.reshape(-1, bp, c)[:, 0:1]
        return jnp.broadcast_to(m, (m.shape[0], bsz, c)).reshape(-1, c)

    p_states = [(meta_state(s0), meta_state(s1)) for (s0, s1) in new_short]
    yp_rows, new_prompt = _trunk(_to_rows(x_prompt, bsz), p_states, layers, bsz)

    y_prompt = _from_rows(yp_rows, bsz, 0, bsz)
    y_sample = _from_rows(ys_rows, bp, 1, n_short)

    def collect(new, b_pad, lo, hi):
        sa = jnp.stack([_from_rows(new[l][0], b_pad, lo, hi) for l in range(0, DEPTH, 2)])
        sb = jnp.stack([_from_rows(new[l][1], b_pad, lo, hi) for l in range(0, DEPTH, 2)])
        sc = jnp.stack([_from_rows(new[l][0], b_pad, lo, hi) for l in range(1, DEPTH, 2)])
        sh = jnp.stack([_from_rows(new[l][1], b_pad, lo, hi)[:, 0] for l in range(1, DEPTH, 2)])
        return sa, sb, sc, sh

    sa_p, sb_p, sc_p, sh_p = collect(new_prompt, bsz, 0, bsz)
    sa_s, sb_s, sc_s, sh_s = collect(new_short, bp, 1, n_short)
    return (y_prompt, y_sample, sa_p, sb_p, sc_p, sh_p, sa_s, sb_s, sc_s, sh_s)
```

```python
import functools
import math

import jax
import jax.numpy as jnp
from jax import lax
from jax.experimental import pallas as pl
from jax.experimental.pallas import tpu as pltpu

D_MODEL = 1024
DEPTH = 4
D_A = D_MODEL // 2
D_B = D_MODEL // 2
W_A = 31
W_B = 3
W_C = 4
D_RNN = D_MODEL
LRU_HEADS = 8
LRU_BW = D_RNN // LRU_HEADS
LRU_C = 8.0
D_FF = 4 * D_MODEL
ALPHA = (2 * DEPTH) ** 0.25
LN_EPS = 1e-5

F32 = jnp.float32
BF16 = jnp.bfloat16

LANES = 128
SUBLANES = 8
BF16_ROWS = 16
VMEM_LIMIT_BYTES = 56 * 1024 * 1024
ROW_TILE_TARGET = 512
MATMUL_CHUNK_TARGET = 256
CONV_CHUNK = 64


def _largest_divisor(n, target, multiple):
    for d in range(min(n, target), 0, -1):
        if n % d == 0 and d % multiple == 0:
            return d
    raise ValueError(f"no divisor of {n} that is a multiple of {multiple}")


def _layer_norm(x, g, b):
    mu = jnp.mean(x, axis=-1, keepdims=True)
    xc = x - mu
    var = jnp.mean(xc * xc, axis=-1, keepdims=True)
    return xc * lax.rsqrt(var + LN_EPS) * g + b


def _bdot(a, b):
    return jnp.dot(a, b, preferred_element_type=F32)


def _mlp_chunk(x1, w1_ref, w2_ref, g2_ref, b2_ref):
    hid = jnp.square(jnp.maximum(_bdot(x1.astype(BF16), w1_ref[...]), 0.0))
    m = _bdot(hid.astype(BF16), w2_ref[...])
    return _layer_norm(ALPHA * x1 + m, g2_ref[...], b2_ref[...])


def _even_layer_kernel(x_ref, sa_ref, sb_ref, w_in_ref, b_in_ref, caw_ref, cab_ref,
                       lag_ref, lab_ref, cbw_ref, w_out_ref, b_out_ref, g1_ref, b1_ref,
                       w1_ref, w2_ref, g2_ref, b2_ref,
                       o_ref, na_ref, nb_ref,
                       u_ext, v_ext, gb_s, ycat, *, B, R, CM, CH, n_tiles):
    HA = (W_A - 1) * B
    HB = (W_B - 1) * B

    @pl.when(pl.program_id(0) == 0)
    def _():
        u_ext[0:HA, :] = sa_ref[...]
        v_ext[0:HB, :] = sb_ref[...]

    def in_proj(r0):
        xb = x_ref[pl.ds(r0, CM), :].astype(BF16)

        def sec(s):
            lo = s * D_A
            return _bdot(xb, w_in_ref[:, lo:lo + D_A]) + b_in_ref[:, lo:lo + D_A]

        u = sec(0) * jax.nn.sigmoid(sec(1))
        u_ext[pl.ds(HA + r0, CM), :] = u
        gb_s[pl.ds(r0, CM), :] = sec(2)
        v = sec(3) * sec(4)
        v_ext[pl.ds(HB + r0, CM), :] = v

    def convs(r0):
        accs = []
        for j in range(D_A // LANES):
            ls = slice(j * LANES, (j + 1) * LANES)
            acc = None
            for k in range(W_A):
                term = caw_ref[k:k + 1, ls] * u_ext[pl.ds(r0 + k * B, CH), ls]
                acc = term if acc is None else acc + term
            accs.append(acc)
        ca = jnp.concatenate(accs, axis=-1) + cab_ref[...]
        y_a = jax.nn.silu(_layer_norm(ca, lag_ref[...], lab_ref[...]))
        ycat[pl.ds(r0, CH), 0:D_A] = y_a.astype(BF16)
        cb = None
        for k in range(W_B):
            term = cbw_ref[k:k + 1, :] * v_ext[pl.ds(r0 + k * B, CH), :]
            cb = term if cb is None else cb + term
        y_b = gb_s[pl.ds(r0, CH), :] * cb
        ycat[pl.ds(r0, CH), D_A:D_A + D_B] = y_b.astype(BF16)

    def out_proj_mlp(r0):
        m = _bdot(ycat[pl.ds(r0, CM), :], w_out_ref[...]) + b_out_ref[...]
        x1 = _layer_norm(ALPHA * x_ref[pl.ds(r0, CM), :] + m, g1_ref[...], b1_ref[...])
        o_ref[pl.ds(r0, CM), :] = _mlp_chunk(x1, w1_ref, w2_ref, g2_ref, b2_ref)

    for c in range(R // CM):
        in_proj(c * CM)
    for c in range(R // CM):
        for s in range(CM // CH):
            convs(c * CM + s * CH)
        out_proj_mlp(c * CM)

    na_ref[...] = u_ext[R:R + HA, :]
    nb_ref[...] = v_ext[R:R + HB, :]
    if n_tiles > 1:
        u_ext[0:HA, :] = u_ext[R:R + HA, :]
        v_ext[0:HB, :] = v_ext[R:R + HB, :]


def _odd_layer_kernel(x_ref, sc_ref, sh_ref, w_in_ref, b_in_ref, ccw_ref, ccb_ref,
                      wg_ref, bga_ref, bgx_ref, lam_ref, w_out_ref, b_out_ref, g1_ref, b1_ref,
                      w1_ref, w2_ref, g2_ref, b2_ref,
                      o_ref, nc_ref, nh_ref,
                      uc_ext, a_s, b_s, gg_s, h_s, *, B, R, CM, n_tiles):
    HC = (W_C - 1) * B

    @pl.when(pl.program_id(0) == 0)
    def _():
        uc_ext[0:HC, :] = sc_ref[...]
        h_s[...] = sh_ref[...]

    def in_proj(r0):
        xb = x_ref[pl.ds(r0, CM), :].astype(BF16)
        gate = _bdot(xb, w_in_ref[:, 0:D_RNN]) + b_in_ref[:, 0:D_RNN]
        gg_s[pl.ds(r0, CM), :] = jax.nn.gelu(gate)
        u = _bdot(xb, w_in_ref[:, D_RNN:2 * D_RNN]) + b_in_ref[:, D_RNN:2 * D_RNN]
        uc_ext[pl.ds(HC + r0, CM), :] = u

    lam = lam_ref[...]
    sp = jnp.maximum(-lam, 0.0) + jnp.log1p(jnp.exp(-jnp.abs(lam)))

    def gates(r0):
        xc = None
        for k in range(W_C):
            term = ccw_ref[k:k + 1, :] * uc_ext[pl.ds(r0 + k * B, CM), :]
            xc = term if xc is None else xc + term
        xc = xc + ccb_ref[...]
        for h in range(LRU_HEADS):
            hs = slice(h * LRU_BW, (h + 1) * LRU_BW)
            xh = xc[:, hs]
            g = _bdot(xh.astype(BF16), wg_ref[h])
            r = jax.nn.sigmoid(g[:, 0:LRU_BW] + bga_ref[:, hs])
            i = jax.nn.sigmoid(g[:, LRU_BW:2 * LRU_BW] + bgx_ref[:, hs])
            log_a = -LRU_C * r * sp[:, hs]
            a = jnp.exp(log_a)
            mult = jnp.sqrt(jnp.maximum(1.0 - a * a, 0.0))
            a_s[pl.ds(r0, CM), hs] = a
            b_s[pl.ds(r0, CM), hs] = mult * (i * xh)

    def scan(r0, h):
        for t in range(CM // B):
            rows = pl.ds(r0 + t * B, B)
            h = a_s[rows, :] * h + b_s[rows, :]
            b_s[rows, :] = h
        return h

    def out_proj_mlp(r0):
        y = (b_s[pl.ds(r0, CM), :] * gg_s[pl.ds(r0, CM), :]).astype(BF16)
        m = _bdot(y, w_out_ref[...]) + b_out_ref[...]
        x1 = _layer_norm(ALPHA * x_ref[pl.ds(r0, CM), :] + m, g1_ref[...], b1_ref[...])
        o_ref[pl.ds(r0, CM), :] = _mlp_chunk(x1, w1_ref, w2_ref, g2_ref, b2_ref)

    for c in range(R // CM):
        in_proj(c * CM)
    h = h_s[...]
    for c in range(R // CM):
        gates(c * CM)
        h = scan(c * CM, h)
        out_proj_mlp(c * CM)
    h_s[...] = h

    nc_ref[...] = uc_ext[R:R + HC, :]
    nh_ref[...] = h
    if n_tiles > 1:
        uc_ext[0:HC, :] = uc_ext[R:R + HC, :]


def _const_spec(shape):
    zeros = (0,) * len(shape)
    return pl.BlockSpec(shape, lambda i: zeros, pipeline_mode=pl.Buffered(1))


def _row_spec(R, width):
    return pl.BlockSpec((R, width), lambda i: (i, 0))


def _tiling(n_rows, B):
    if n_rows <= 2 * ROW_TILE_TARGET:
        R = n_rows
    else:
        R = _largest_divisor(n_rows, ROW_TILE_TARGET, math.lcm(B, BF16_ROWS))
    n_tiles = n_rows // R
    CM = _largest_divisor(R, MATMUL_CHUNK_TARGET, math.lcm(B, BF16_ROWS))
    return R, CM, n_tiles


def _params():
    return pltpu.CompilerParams(dimension_semantics=("arbitrary",),
                                vmem_limit_bytes=VMEM_LIMIT_BYTES)


_MLP_KEYS = ('w_mlp1', 'w_mlp2', 'ln2_g', 'ln2_b')


def _even_layer(x, sa, sb, prm, B):
    n_rows = x.shape[0]
    R, CM, n_tiles = _tiling(n_rows, B)
    CH = _largest_divisor(CM, CONV_CHUNK, BF16_ROWS)
    HA, HB = (W_A - 1) * B, (W_B - 1) * B
    assert n_tiles == 1 or R >= HA
    consts = [sa, sb] + [prm[k] for k in ('w_in', 'b_in', 'conv_a_w', 'conv_a_b', 'ln_a_g', 'ln_a_b',
                                          'conv_b_w', 'w_out', 'b_out', 'ln1_g', 'ln1_b') + _MLP_KEYS]
    kern = functools.partial(_even_layer_kernel, B=B, R=R, CM=CM, CH=CH, n_tiles=n_tiles)
    return pl.pallas_call(
        kern,
        grid=(n_tiles,),
        in_specs=[_row_spec(R, D_MODEL)] + [_const_spec(c.shape) for c in consts],
        out_specs=[_row_spec(R, D_MODEL), _const_spec((HA, D_A)), _const_spec((HB, D_B))],
        out_shape=[jax.ShapeDtypeStruct((n_rows, D_MODEL), F32),
                   jax.ShapeDtypeStruct((HA, D_A), F32),
                   jax.ShapeDtypeStruct((HB, D_B), F32)],
        scratch_shapes=[pltpu.VMEM((HA + R, D_A), F32), pltpu.VMEM((HB + R, D_B), F32),
                        pltpu.VMEM((R, D_B), F32), pltpu.VMEM((R, D_A + D_B), BF16)],
        compiler_params=_params(),
        name="even_layer",
    )(x, *consts)


def _odd_layer(x, sc, sh, prm, B):
    n_rows = x.shape[0]
    R, CM, n_tiles = _tiling(n_rows, B)
    HC = (W_C - 1) * B
    assert n_tiles == 1 or R >= HC
    consts = [sc, sh] + [prm[k] for k in ('w_in', 'b_in', 'conv_c_w', 'conv_c_b', 'w_gate', 'b_gate_a',
                                          'b_gate_x', 'lru_lambda', 'w_out', 'b_out', 'ln1_g',
                                          'ln1_b') + _MLP_KEYS]
    kern = functools.partial(_odd_layer_kernel, B=B, R=R, CM=CM, n_tiles=n_tiles)
    return pl.pallas_call(
        kern,
        grid=(n_tiles,),
        in_specs=[_row_spec(R, D_MODEL)] + [_const_spec(c.shape) for c in consts],
        out_specs=[_row_spec(R, D_MODEL), _const_spec((HC, D_RNN)), _const_spec((B, D_RNN))],
        out_shape=[jax.ShapeDtypeStruct((n_rows, D_MODEL), F32),
                   jax.ShapeDtypeStruct((HC, D_RNN), F32),
                   jax.ShapeDtypeStruct((B, D_RNN), F32)],
        scratch_shapes=[pltpu.VMEM((HC + R, D_RNN), F32), pltpu.VMEM((R, D_RNN), F32),
                        pltpu.VMEM((R, D_RNN), F32), pltpu.VMEM((R, D_RNN), F32),
                        pltpu.VMEM((B, D_RNN), F32)],
        compiler_params=_params(),
        name="odd_layer",
    )(x, *consts)


def _trunk(x, states, layers, B):
    new_states = []
    for l in range(DEPTH):
        layer = _even_layer if l % 2 == 0 else _odd_layer
        x, s0, s1 = layer(x, states[l][0], states[l][1], layers[l], B)
        new_states.append((s0, s1))
    return x, new_states


def _to_rows(a, b_pad):
    bsz, t, c = a.shape
    a = jnp.transpose(a, (1, 0, 2))
    if b_pad > bsz:
        a = jnp.pad(a, ((0, 0), (0, b_pad - bsz), (0, 0)))
    return a.reshape(t * b_pad, c)


def _from_rows(rows, b_pad, lo, hi):
    c = rows.shape[-1]
    a = rows.reshape(-1, b_pad, c)[:, lo:hi]
    return jnp.transpose(a, (1, 0, 2))


def kernel(x_prompt, x_sample, state_conv_a, state_conv_b, state_conv_c, state_lru, meta_tokens,
           ln1_g, ln1_b, ln2_g, ln2_b,
           w_in_e, b_in_e, conv_a_w, conv_a_b, ln_a_g, ln_a_b, conv_b_w, w_out_e, b_out_e,
           w_in_o, b_in_o, conv_c_w, conv_c_b, w_gate_a, b_gate_a, w_gate_x, b_gate_x, lru_lambda,
           w_out_o, b_out_o, w_mlp1, w_mlp2):
    bsz, seq, _ = x_prompt.shape
    dec_b, dec_t, _ = x_sample.shape
    n_meta = meta_tokens.shape[0]
    assert dec_t == n_meta, "meta tokens ride along with the sample streams as one more sequence"
    assert bsz % SUBLANES == 0

    row = lambda v: v.reshape(1, -1)
    layers = []
    for l in range(DEPTH):
        i = l // 2
        prm = dict(ln1_g=row(ln1_g[l]), ln1_b=row(ln1_b[l]), ln2_g=row(ln2_g[l]), ln2_b=row(ln2_b[l]),
                   w_mlp1=w_mlp1[l].astype(BF16), w_mlp2=w_mlp2[l].astype(BF16))
        if l % 2 == 0:
            prm.update(w_in=w_in_e[i].astype(BF16), b_in=row(b_in_e[i]), conv_a_w=conv_a_w[i],
                       conv_a_b=row(conv_a_b[i]), ln_a_g=row(ln_a_g[i]), ln_a_b=row(ln_a_b[i]),
                       conv_b_w=conv_b_w[i], w_out=w_out_e[i].astype(BF16), b_out=row(b_out_e[i]))
        else:
            prm.update(w_in=w_in_o[i].astype(BF16), b_in=row(b_in_o[i]), conv_c_w=conv_c_w[i],
                       conv_c_b=row(conv_c_b[i]),
                       w_gate=jnp.concatenate([w_gate_a[i], w_gate_x[i]], axis=-1).astype(BF16),
                       b_gate_a=row(b_gate_a[i]), b_gate_x=row(b_gate_x[i]),
                       lru_lambda=row(lru_lambda[i]), w_out=w_out_o[i].astype(BF16),
                       b_out=row(b_out_o[i]))
        layers.append(prm)

    n_short = dec_b + 1
    bp = -(-n_short // SUBLANES) * SUBLANES
    xs = jnp.concatenate([meta_tokens[None].astype(x_sample.dtype), x_sample], axis=0)
    with_zero = lambda s: jnp.concatenate([jnp.zeros_like(s[:1]), s], axis=0)
    states = []
    for l in range(DEPTH):
        i = l // 2
        if l % 2 == 0:
            states.append((_to_rows(with_zero(state_conv_a[i]), bp), _to_rows(with_zero(state_conv_b[i]), bp)))
        else:
            states.append((_to_rows(with_zero(state_conv_c[i]), bp),
                           _to_rows(with_zero(state_lru[i])[:, None, :], bp)))
    ys_rows, new_short = _trunk(_to_rows(xs, bp), states, layers, bp)

    def meta_state(rows):
        c = rows.shape[-1]
        m = rows.reshape(-1, bp, c)[:, 0:1]
        return jnp.broadcast_to(m, (m.shape[0], bsz, c)).reshape(-1, c)

    p_states = [(meta_state(s0), meta_state(s1)) for (s0, s1) in new_short]
    yp_rows, new_prompt = _trunk(_to_rows(x_prompt, bsz), p_states, layers, bsz)

    y_prompt = _from_rows(yp_rows, bsz, 0, bsz)
    y_sample = _from_rows(ys_rows, bp, 1, n_short)

    def collect(new, b_pad, lo, hi):
        sa = jnp.stack([_from_rows(new[l][0], b_pad, lo, hi) for l in range(0, DEPTH, 2)])
        sb = jnp.stack([_from_rows(new[l][1], b_pad, lo, hi) for l in range(0, DEPTH, 2)])
        sc = jnp.stack([_from_rows(new[l][0], b_pad, lo, hi) for l in range(1, DEPTH, 2)])
        sh = jnp.stack([_from_rows(new[l][1], b_pad, lo, hi)[:, 0] for l in range(1, DEPTH, 2)])
        return sa, sb, sc, sh

    sa_p, sb_p, sc_p, sh_p = collect(new_prompt, bsz, 0, bsz)
    sa_s, sb_s, sc_s, sh_s = collect(new_short, bp, 1, n_short)
    return (y_prompt, y_sample, sa_p, sb_p, sc_p, sh_p, sa_s, sb_s, sc_s, sh_s)
```

```python
import functools
import math

import jax
import jax.numpy as jnp
from jax import lax
from jax.experimental import pallas as pl
from jax.experimental.pallas import tpu as pltpu

D_MODEL = 1024
DEPTH = 4
D_A = D_MODEL // 2
D_B = D_MODEL // 2
W_A = 31
W_B = 3
W_C = 4
D_RNN = D_MODEL
LRU_HEADS = 8
LRU_BW = D_RNN // LRU_HEADS
LRU_C = 8.0
D_FF = 4 * D_MODEL
ALPHA = (2 * DEPTH) ** 0.25
LN_EPS = 1e-5

F32 = jnp.float32
BF16 = jnp.bfloat16

LANES = 128
SUBLANES = 8
BF16_ROWS = 16
VMEM_LIMIT_BYTES = 60 * 1024 * 1024
ROW_TILE_TARGET = 512
SINGLE_TILE_MAX_ROWS = 1024
MATMUL_CHUNK_TARGET = 256
N_SLICE = 512
CONV_CHUNK = 64
CONV_GROUP = 16


def _largest_divisor(n, target, multiple):
    for d in range(min(n, target), 0, -1):
        if n % d == 0 and d % multiple == 0:
            return d
    raise ValueError(f"no divisor of {n} that is a multiple of {multiple}")


def _layer_norm(x, g, b):
    mu = jnp.mean(x, axis=-1, keepdims=True)
    xc = x - mu
    var = jnp.mean(xc * xc, axis=-1, keepdims=True)
    return xc * lax.rsqrt(var + LN_EPS) * g + b


def _bdot(a, b):
    return jnp.dot(a, b, preferred_element_type=F32)


def _ndot(a, w_ref):
    n = w_ref.shape[1]
    return jnp.concatenate([_bdot(a, w_ref[:, j:j + N_SLICE]) for j in range(0, n, N_SLICE)], axis=-1)


def _ordered_zero(v):
    sixteen = jnp.uint32(16)
    bits = pltpu.bitcast(v, jnp.uint32)
    bits = lax.shift_right_logical(lax.shift_right_logical(bits, sixteen), sixteen)
    return pltpu.bitcast(bits, F32)


def _mlp_chunk(x1, w1_ref, w2_ref, g2_ref, b2_ref):
    x1b = x1.astype(BF16)
    rows = x1.shape[0]
    hid, anchors = [], []
    for j in range(0, w1_ref.shape[1], N_SLICE):
        hj = jnp.square(jnp.maximum(_bdot(x1b, w1_ref[:, j:j + N_SLICE]), 0.0))
        anchors.append(_ordered_zero(hj[0:SUBLANES, 0:LANES] + hj[rows - SUBLANES:rows, N_SLICE - LANES:N_SLICE]))
        hid.append(hj.astype(BF16))
    m = _ndot(jnp.concatenate(hid, axis=-1), w2_ref)
    return _layer_norm(ALPHA * x1 + m, g2_ref[...], b2_ref[...]), anchors


def _run_layer(front, vector_stage, write_state, x_ref, xp_ref, y_s, o_ref, proj_refs, mlp_refs,
               R, CM, n_tiles):
    w_out_ref, b_out_ref, g1_ref, b1_ref = proj_refs
    n_chunks = R // CM

    def post_ln1(y, x):
        m = _ndot(y, w_out_ref) + b_out_ref[...]
        return _layer_norm(ALPHA * x + m, g1_ref[...], b1_ref[...])

    if n_tiles == 1:
        for c in range(n_chunks):
            front(c * CM)
        for c in range(n_chunks):
            vector_stage(c * CM, None, 0)
        for c in range(n_chunks):
            rows = pl.ds(c * CM, CM)
            o_ref[rows, :] = _mlp_chunk(post_ln1(y_s[0, rows, :], x_ref[rows, :]), *mlp_refs)[0]
        write_state()
        return

    step = pl.program_id(0)
    wr = lax.rem(step, 2)
    rd = 1 - wr

    @pl.when(step == 0)
    def _():
        y_s[1] = jnp.zeros(y_s.shape[1:], BF16)

    y_prev = [y_s[rd, pl.ds(c * CM, CM), :] for c in range(n_chunks)]
    for c in range(n_chunks):
        rows = pl.ds(c * CM, CM)
        x1 = post_ln1(y_prev[c], xp_ref[rows, :])
        front(c * CM)
        o_ref[rows, :], anchors = _mlp_chunk(x1, *mlp_refs)
        vector_stage(c * CM, anchors, wr)

    pl.when(step < n_tiles)(write_state)


def _even_layer_kernel(x_ref, xp_ref, sa_ref, sb_ref, w_in_ref, b_in_ref, caw_ref, cab_ref,
                       lag_ref, lab_ref, cbw_ref, w_out_ref, b_out_ref, g1_ref, b1_ref,
                       w1_ref, w2_ref, g2_ref, b2_ref,
                       o_ref, na_ref, nb_ref,
                       u_ext, v_ext, gb_s, ca_s, y_s, *, B, R, CM, CH, G, n_tiles):
    HA = (W_A - 1) * B
    HB = (W_B - 1) * B

    @pl.when(pl.program_id(0) == 0)
    def _():
        u_ext[0:HA, :] = sa_ref[...]
        v_ext[0:HB, :] = sb_ref[...]

    def in_proj(r0):
        xb = x_ref[pl.ds(r0, CM), :].astype(BF16)

        def sec(s):
            lo = s * D_A
            return _bdot(xb, w_in_ref[:, lo:lo + D_A]) + b_in_ref[:, lo:lo + D_A]

        u = sec(0) * jax.nn.sigmoid(sec(1))
        u_ext[pl.ds(HA + r0, CM), :] = u
        gb_s[pl.ds(r0, CM), :] = sec(2)
        v = sec(3) * sec(4)
        v_ext[pl.ds(HB + r0, CM), :] = v

    def conv_a(r0, anchors):
        zero = None
        n_group = 0
        for j in range(D_A // LANES):
            ls = slice(j * LANES, (j + 1) * LANES)
            wb = [jnp.broadcast_to(caw_ref[k:k + 1, ls], (SUBLANES, LANES)) for k in range(W_A)]
            for q in range(0, B, SUBLANES):
                for t0 in range(0, CM // B, G):
                    base = r0 + t0 * B + q
                    if anchors is not None:
                        a = anchors[min(n_group, len(anchors) - 1)]
                        zero = a if zero is None else zero + a
                    n_group += 1
                    wg = wb if zero is None else [w + zero for w in wb]
                    accs = [None] * G
                    for m in range(G + W_A - 1):
                        xin = u_ext[pl.ds(base + m * B, SUBLANES), ls]
                        for i in range(max(0, m - (W_A - 1)), min(G, m + 1)):
                            term = wg[m - i] * xin
                            accs[i] = term if accs[i] is None else accs[i] + term
                    for i in range(G):
                        ca_s[pl.ds(base + i * B, SUBLANES), ls] = accs[i]
                    zero = _ordered_zero(accs[G - 1])

    def gating(r0, slot):
        ca = ca_s[pl.ds(r0, CH), :] + cab_ref[...]
        y_a = jax.nn.silu(_layer_norm(ca, lag_ref[...], lab_ref[...]))
        y_s[slot, pl.ds(r0, CH), 0:D_A] = y_a.astype(BF16)
        cb = None
        for k in range(W_B):
            term = cbw_ref[k:k + 1, :] * v_ext[pl.ds(r0 + k * B, CH), :]
            cb = term if cb is None else cb + term
        y_b = gb_s[pl.ds(r0, CH), :] * cb
        y_s[slot, pl.ds(r0, CH), D_A:D_A + D_B] = y_b.astype(BF16)

    def vector_stage(r0, anchors, slot):
        conv_a(r0, anchors)
        for s in range(CM // CH):
            gating(r0 + s * CH, slot)

    def write_state():
        na_ref[...] = u_ext[R:R + HA, :]
        nb_ref[...] = v_ext[R:R + HB, :]
        if n_tiles > 1:
            u_ext[0:HA, :] = u_ext[R:R + HA, :]
            v_ext[0:HB, :] = v_ext[R:R + HB, :]

    _run_layer(in_proj, vector_stage, write_state, x_ref, xp_ref, y_s, o_ref,
               (w_out_ref, b_out_ref, g1_ref, b1_ref), (w1_ref, w2_ref, g2_ref, b2_ref), R, CM, n_tiles)


def _odd_layer_kernel(x_ref, xp_ref, sc_ref, sh_ref, w_in_ref, b_in_ref, ccw_ref, ccb_ref,
                      wg_ref, bga_ref, bgx_ref, lam_ref, w_out_ref, b_out_ref, g1_ref, b1_ref,
                      w1_ref, w2_ref, g2_ref, b2_ref,
                      o_ref, nc_ref, nh_ref,
                      uc_ext, g_s, a_s, b_s, gg_s, h_s, y_s, *, B, R, CM, n_tiles):
    HC = (W_C - 1) * B

    @pl.when(pl.program_id(0) == 0)
    def _():
        uc_ext[0:HC, :] = sc_ref[...]
        h_s[...] = sh_ref[...]

    def front(r0):
        rows = pl.ds(r0, CM)
        xb = x_ref[rows, :].astype(BF16)
        gate = _bdot(xb, w_in_ref[:, 0:D_RNN]) + b_in_ref[:, 0:D_RNN]
        gg_s[rows, :] = jax.nn.gelu(gate)
        u = _bdot(xb, w_in_ref[:, D_RNN:2 * D_RNN]) + b_in_ref[:, D_RNN:2 * D_RNN]
        uc_ext[pl.ds(HC + r0, CM), :] = u
        xc = None
        for k in range(W_C):
            term = ccw_ref[k:k + 1, :] * uc_ext[pl.ds(r0 + k * B, CM), :]
            xc = term if xc is None else xc + term
        xc = xc + ccb_ref[...]
        b_s[rows, :] = xc
        for h in range(LRU_HEADS):
            hs = slice(h * LRU_BW, (h + 1) * LRU_BW)
            g_s[rows, 2 * h * LRU_BW:2 * (h + 1) * LRU_BW] = _bdot(xc[:, hs].astype(BF16), wg_ref[h])

    lam = lam_ref[...]
    sp = jnp.maximum(-lam, 0.0) + jnp.log1p(jnp.exp(-jnp.abs(lam)))

    def gates(r0, anchors):
        rows = pl.ds(r0, CM)
        for h in range(LRU_HEADS):
            hs = slice(h * LRU_BW, (h + 1) * LRU_BW)
            g = g_s[rows, 2 * h * LRU_BW:2 * (h + 1) * LRU_BW]
            bga = bga_ref[:, hs]
            bgx = bgx_ref[:, hs]
            if anchors is not None:
                anchor = anchors[min(h, len(anchors) - 1)]
                bga = bga + anchor[0:1, :]
                bgx = bgx + anchor[0:1, :]
            r = jax.nn.sigmoid(g[:, 0:LRU_BW] + bga)
            i = jax.nn.sigmoid(g[:, LRU_BW:2 * LRU_BW] + bgx)
            log_a = -LRU_C * r * sp[:, hs]
            a = jnp.exp(log_a)
            mult = jnp.sqrt(jnp.maximum(1.0 - a * a, 0.0))
            a_s[rows, hs] = a
            b_s[rows, hs] = mult * (i * b_s[rows, hs])

    h_carry = [None]

    def vector_stage(r0, anchors, slot):
        gates(r0, anchors)
        h = h_s[...] if h_carry[0] is None else h_carry[0]
        for t in range(CM // B):
            rows = pl.ds(r0 + t * B, B)
            h = a_s[rows, :] * h + b_s[rows, :]
            b_s[rows, :] = h
        h_carry[0] = h
        rows = pl.ds(r0, CM)
        y_s[slot, rows, :] = (b_s[rows, :] * gg_s[rows, :]).astype(BF16)

    def write_state():
        nc_ref[...] = uc_ext[R:R + HC, :]
        nh_ref[...] = h_carry[0]
        if n_tiles > 1:
            h_s[...] = h_carry[0]
            uc_ext[0:HC, :] = uc_ext[R:R + HC, :]

    _run_layer(front, vector_stage, write_state, x_ref, xp_ref, y_s, o_ref,
               (w_out_ref, b_out_ref, g1_ref, b1_ref), (w1_ref, w2_ref, g2_ref, b2_ref), R, CM, n_tiles)


def _const_spec(shape):
    zeros = (0,) * len(shape)
    return pl.BlockSpec(shape, lambda i: zeros, pipeline_mode=pl.Buffered(1))


def _grid_and_row_specs(R, n_tiles):
    if n_tiles == 1:
        spec = pl.BlockSpec((R, D_MODEL), lambda i: (i, 0))
        return (1,), spec, spec, spec
    in_spec = pl.BlockSpec((R, D_MODEL), lambda i: (jnp.minimum(i, n_tiles - 1), 0))
    lag_spec = pl.BlockSpec((R, D_MODEL), lambda i: (jnp.maximum(i - 1, 0), 0))
    return (n_tiles + 1,), in_spec, lag_spec, lag_spec


def _y_scratch(R, n_tiles):
    return pltpu.VMEM((2 if n_tiles > 1 else 1, R, D_MODEL), BF16)


def _tiling(n_rows, B):
    if n_rows <= SINGLE_TILE_MAX_ROWS:
        R = n_rows
    else:
        R = _largest_divisor(n_rows, ROW_TILE_TARGET, math.lcm(B, BF16_ROWS))
    n_tiles = n_rows // R
    CM = _largest_divisor(R, MATMUL_CHUNK_TARGET, math.lcm(B, BF16_ROWS))
    return R, CM, n_tiles


def _params():
    return pltpu.CompilerParams(dimension_semantics=("arbitrary",),
                                vmem_limit_bytes=VMEM_LIMIT_BYTES)


_MLP_KEYS = ('w_mlp1', 'w_mlp2', 'ln2_g', 'ln2_b')


def _even_layer(x, sa, sb, prm, B):
    n_rows = x.shape[0]
    R, CM, n_tiles = _tiling(n_rows, B)
    CH = _largest_divisor(CM, CONV_CHUNK, BF16_ROWS)
    G = _largest_divisor(CM // B, CONV_GROUP, 1)
    HA, HB = (W_A - 1) * B, (W_B - 1) * B
    assert n_tiles == 1 or R >= HA
    consts = [sa, sb] + [prm[k] for k in ('w_in', 'b_in', 'conv_a_w', 'conv_a_b', 'ln_a_g', 'ln_a_b',
                                          'conv_b_w', 'w_out', 'b_out', 'ln1_g', 'ln1_b') + _MLP_KEYS]
    kern = functools.partial(_even_layer_kernel, B=B, R=R, CM=CM, CH=CH, G=G, n_tiles=n_tiles)
    grid, x_spec, xp_spec, o_spec = _grid_and_row_specs(R, n_tiles)
    return pl.pallas_call(
        kern,
        grid=grid,
        in_specs=[x_spec, xp_spec] + [_const_spec(c.shape) for c in consts],
        out_specs=[o_spec, _const_spec((HA, D_A)), _const_spec((HB, D_B))],
        out_shape=[jax.ShapeDtypeStruct((n_rows, D_MODEL), F32),
                   jax.ShapeDtypeStruct((HA, D_A), F32),
                   jax.ShapeDtypeStruct((HB, D_B), F32)],
        scratch_shapes=[pltpu.VMEM((HA + R, D_A), F32), pltpu.VMEM((HB + R, D_B), F32),
                        pltpu.VMEM((R, D_B), F32), pltpu.VMEM((R, D_A), F32), _y_scratch(R, n_tiles)],
        compiler_params=_params(),
        name="even_layer",
    )(x, x, *consts)


def _odd_layer(x, sc, sh, prm, B):
    n_rows = x.shape[0]
    R, CM, n_tiles = _tiling(n_rows, B)
    HC = (W_C - 1) * B
    assert n_tiles == 1 or R >= HC
    consts = [sc, sh] + [prm[k] for k in ('w_in', 'b_in', 'conv_c_w', 'conv_c_b', 'w_gate', 'b_gate_a',
                                          'b_gate_x', 'lru_lambda', 'w_out', 'b_out', 'ln1_g',
                                          'ln1_b') + _MLP_KEYS]
    kern = functools.partial(_odd_layer_kernel, B=B, R=R, CM=CM, n_tiles=n_tiles)
    grid, x_spec, xp_spec, o_spec = _grid_and_row_specs(R, n_tiles)
    rows_f32 = lambda width: pltpu.VMEM((R, width), F32)
    return pl.pallas_call(
        kern,
        grid=grid,
        in_specs=[x_spec, xp_spec] + [_const_spec(c.shape) for c in consts],
        out_specs=[o_spec, _const_spec((HC, D_RNN)), _const_spec((B, D_RNN))],
        out_shape=[jax.ShapeDtypeStruct((n_rows, D_MODEL), F32),
                   jax.ShapeDtypeStruct((HC, D_RNN), F32),
                   jax.ShapeDtypeStruct((B, D_RNN), F32)],
        scratch_shapes=[pltpu.VMEM((HC + R, D_RNN), F32), rows_f32(2 * D_RNN),
                        rows_f32(D_RNN), rows_f32(D_RNN), rows_f32(D_RNN),
                        pltpu.VMEM((B, D_RNN), F32), _y_scratch(R, n_tiles)],
        compiler_params=_params(),
        name="odd_layer",
    )(x, x, *consts)


def _trunk(x, states, layers, B):
    new_states = []
    for l in range(DEPTH):
        layer = _even_layer if l % 2 == 0 else _odd_layer
        x, s0, s1 = layer(x, states[l][0], states[l][1], layers[l], B)
        new_states.append((s0, s1))
    return x, new_states


def _to_rows(a, b_pad):
    bsz, t, c = a.shape
    a = jnp.transpose(a, (1, 0, 2))
    if b_pad > bsz:
        a = jnp.pad(a, ((0, 0), (0, b_pad - bsz), (0, 0)))
    return a.reshape(t * b_pad, c)


def _from_rows(rows, b_pad, lo, hi):
    c = rows.shape[-1]
    a = rows.reshape(-1, b_pad, c)[:, lo:hi]
    return jnp.transpose(a, (1, 0, 2))


def kernel(x_prompt, x_sample, state_conv_a, state_conv_b, state_conv_c, state_lru, meta_tokens,
           ln1_g, ln1_b, ln2_g, ln2_b,
           w_in_e, b_in_e, conv_a_w, conv_a_b, ln_a_g, ln_a_b, conv_b_w, w_out_e, b_out_e,
           w_in_o, b_in_o, conv_c_w, conv_c_b, w_gate_a, b_gate_a, w_gate_x, b_gate_x, lru_lambda,
           w_out_o, b_out_o, w_mlp1, w_mlp2):
    bsz, seq, _ = x_prompt.shape
    dec_b, dec_t, _ = x_sample.shape
    n_meta = meta_tokens.shape[0]
    assert dec_t == n_meta, "meta tokens ride along with the sample streams as one more sequence"
    assert bsz % SUBLANES == 0

    row = lambda v: v.reshape(1, -1)
    layers = []
    for l in range(DEPTH):
        i = l // 2
        prm = dict(ln1_g=row(ln1_g[l]), ln1_b=row(ln1_b[l]), ln2_g=row(ln2_g[l]), ln2_b=row(ln2_b[l]),
                   w_mlp1=w_mlp1[l].astype(BF16), w_mlp2=w_mlp2[l].astype(BF16))
        if l % 2 == 0:
            prm.update(w_in=w_in_e[i].astype(BF16), b_in=row(b_in_e[i]), conv_a_w=conv_a_w[i],
                       conv_a_b=row(conv_a_b[i]), ln_a_g=row(ln_a_g[i]), ln_a_b=row(ln_a_b[i]),
                       conv_b_w=conv_b_w[i], w_out=w_out_e[i].astype(BF16), b_out=row(b_out_e[i]))
        else:
            prm.update(w_in=w_in_o[i].astype(BF16), b_in=row(b_in_o[i]), conv_c_w=conv_c_w[i],
                       conv_c_b=row(conv_c_b[i]),
                       w_gate=jnp.concatenate([w_gate_a[i], w_gate_x[i]], axis=-1).astype(BF16),
                       b_gate_a=row(b_gate_a[i]), b_gate_x=row(b_gate_x[i]),
                       lru_lambda=row(lru_lambda[i]), w_out=w_out_o[i].astype(BF16),
                       b_out=row(b_out_o[i]))
        layers.append(prm)

    n_short = dec_b + 1
    bp = -(-n_short // SUBLANES) * SUBLANES
    xs = jnp.concatenate([meta_tokens[None].astype(x_sample.dtype), x_sample], axis=0)
    with_zero = lambda s: jnp.concatenate([jnp.zeros_like(s[:1]), s], axis=0)
    states = []
    for l in range(DEPTH):
        i = l // 2
        if l % 2 == 0:
            states.append((_to_rows(with_zero(state_conv_a[i]), bp), _to_rows(with_zero(state_conv_b[i]), bp)))
        else:
            states.append((_to_rows(with_zero(state_conv_c[i]), bp),
                           _to_rows(with_zero(state_lru[i])[:, None, :], bp)))
    ys_rows, new_short = _trunk(_to_rows(xs, bp), states, layers, bp)

    def meta_state(rows):
        c = rows.shape[-1]
        m = rows.reshape(-1, bp, c)[:, 0:1]
        return jnp.broadcast_to(m, (m.shape[0], bsz, c)).reshape(-1, c)

    p_states = [(meta_state(s0), meta_state(s1)) for (s0, s1) in new_short]
    yp_rows, new_prompt = _trunk(_to_rows(x_prompt, bsz), p_states, layers, bsz)

    y_prompt = _from_rows(yp_rows, bsz, 0, bsz)
    y_sample = _from_rows(ys_rows, bp, 1, n_short)

    def collect(new, b_pad, lo, hi):
        sa = jnp.stack([_from_rows(new[l][0], b_pad, lo, hi) for l in range(0, DEPTH, 2)])
        sb = jnp.stack([_from_rows(new[l][1], b_pad, lo, hi) for l in range(0, DEPTH, 2)])
        sc = jnp.stack([_from_rows(new[l][0], b_pad, lo, hi) for l in range(1, DEPTH, 2)])
        sh = jnp.stack([_from_rows(new[l][1], b_pad, lo, hi)[:, 0] for l in range(1, DEPTH, 2)])
        return sa, sb, sc, sh

    sa_p, sb_p, sc_p, sh_p = collect(new_prompt, bsz, 0, bsz)
    sa_s, sb_s, sc_s, sh_s = collect(new_short, bp, 1, n_short)
    return (y_prompt, y_sample, sa_p, sb_p, sc_p, sh_p, sa_s, sb_s, sc_s, sh_s)
```

```python
import functools
import math

import jax
import jax.numpy as jnp
from jax import lax
from jax.experimental import pallas as pl
from jax.experimental.pallas import tpu as pltpu

D_MODEL = 1024
DEPTH = 4
D_A = D_MODEL // 2
D_B = D_MODEL // 2
W_A = 31
W_B = 3
W_C = 4
D_RNN = D_MODEL
LRU_HEADS = 8
LRU_BW = D_RNN // LRU_HEADS
LRU_C = 8.0
D_FF = 4 * D_MODEL
ALPHA = (2 * DEPTH) ** 0.25
LN_EPS = 1e-5

F32 = jnp.float32
BF16 = jnp.bfloat16

LANES = 128
SUBLANES = 8
BF16_ROWS = 16
VMEM_LIMIT_BYTES = 60 * 1024 * 1024
ROW_TILE_TARGET = 512
SINGLE_TILE_MAX_ROWS = 1024
MATMUL_CHUNK_TARGET = 256
N_SLICE = 512
CONV_CHUNK = 64
CONV_GROUP = 16


def _largest_divisor(n, target, multiple):
    for d in range(min(n, target), 0, -1):
        if n % d == 0 and d % multiple == 0:
            return d
    raise ValueError(f"no divisor of {n} that is a multiple of {multiple}")


def _layer_norm(x, g, b):
    mu = jnp.mean(x, axis=-1, keepdims=True)
    xc = x - mu
    var = jnp.mean(xc * xc, axis=-1, keepdims=True)
    return xc * lax.rsqrt(var + LN_EPS) * g + b


def _bdot(a, b):
    return jnp.dot(a, b, preferred_element_type=F32)


def _ndot(a, w_ref):
    n = w_ref.shape[1]
    return jnp.concatenate([_bdot(a, w_ref[:, j:j + N_SLICE]) for j in range(0, n, N_SLICE)], axis=-1)


def _ordered_zero(v):
    sixteen = jnp.uint32(16)
    bits = pltpu.bitcast(v, jnp.uint32)
    bits = lax.shift_right_logical(lax.shift_right_logical(bits, sixteen), sixteen)
    return pltpu.bitcast(bits, F32)


def _rows_reader(ref, batch_major):
    if not batch_major:
        return lambda r0, n: ref[pl.ds(r0, n), :]

    def read(r0, n):
        v = ref[:, pl.ds(r0 // SUBLANES, n // SUBLANES), :]
        return pltpu.einshape("btd->tbd", v).reshape(n, v.shape[-1])
    return read


def _rows_writer(ref, batch_major):
    if not batch_major:
        def write(r0, value):
            ref[pl.ds(r0, value.shape[0]), :] = value
        return write

    def write(r0, value):
        n, d = value.shape
        v = pltpu.einshape("tbd->btd", value.reshape(n // SUBLANES, SUBLANES, d))
        ref[:, pl.ds(r0 // SUBLANES, n // SUBLANES), :] = v
    return write


def _mlp_chunk(x1, w1_ref, w2_ref, g2_ref, b2_ref):
    x1b = x1.astype(BF16)
    rows = x1.shape[0]
    hid, anchors = [], []
    for j in range(0, w1_ref.shape[1], N_SLICE):
        hj = jnp.square(jnp.maximum(_bdot(x1b, w1_ref[:, j:j + N_SLICE]), 0.0))
        anchors.append(_ordered_zero(hj[0:SUBLANES, 0:LANES] + hj[rows - SUBLANES:rows, N_SLICE - LANES:N_SLICE]))
        hid.append(hj.astype(BF16))
    m = _ndot(jnp.concatenate(hid, axis=-1), w2_ref)
    return _layer_norm(ALPHA * x1 + m, g2_ref[...], b2_ref[...]), anchors


def _run_layer(front, vector_stage, write_state, read_x, read_xp, y_s, write_o, proj_refs, mlp_refs,
               R, CM, n_tiles):
    w_out_ref, b_out_ref, g1_ref, b1_ref = proj_refs
    n_chunks = R // CM

    def post_ln1(y, x):
        m = _ndot(y, w_out_ref) + b_out_ref[...]
        return _layer_norm(ALPHA * x + m, g1_ref[...], b1_ref[...])

    if n_tiles == 1:
        for c in range(n_chunks):
            front(c * CM)
        for c in range(n_chunks):
            vector_stage(c * CM, None, 0)
        for c in range(n_chunks):
            x1 = post_ln1(y_s[0, pl.ds(c * CM, CM), :], read_x(c * CM, CM))
            write_o(c * CM, _mlp_chunk(x1, *mlp_refs)[0])
        write_state()
        return

    step = pl.program_id(0)
    wr = lax.rem(step, 2)
    rd = 1 - wr

    @pl.when(step == 0)
    def _():
        y_s[1] = jnp.zeros(y_s.shape[1:], BF16)

    y_prev = [y_s[rd, pl.ds(c * CM, CM), :] for c in range(n_chunks)]
    for c in range(n_chunks):
        x1 = post_ln1(y_prev[c], read_xp(c * CM, CM))
        front(c * CM)
        out, anchors = _mlp_chunk(x1, *mlp_refs)
        write_o(c * CM, out)
        vector_stage(c * CM, anchors, wr)

    pl.when(step < n_tiles)(write_state)


def _even_layer_kernel(x_ref, xp_ref, sa_ref, sb_ref, w_in_ref, b_in_ref, caw_ref, cab_ref,
                       lag_ref, lab_ref, cbw_ref, w_out_ref, b_out_ref, g1_ref, b1_ref,
                       w1_ref, w2_ref, g2_ref, b2_ref,
                       o_ref, na_ref, nb_ref,
                       u_ext, v_ext, gb_s, ca_s, y_s, *, B, R, CM, CH, G, n_tiles, x_batch_major):
    HA = (W_A - 1) * B
    HB = (W_B - 1) * B

    @pl.when(pl.program_id(0) == 0)
    def _():
        u_ext[0:HA, :] = sa_ref[...]
        v_ext[0:HB, :] = sb_ref[...]

    read_x = _rows_reader(x_ref, x_batch_major)
    read_xp = _rows_reader(xp_ref, x_batch_major)

    def in_proj(r0):
        xb = read_x(r0, CM).astype(BF16)

        def sec(s):
            lo = s * D_A
            return _bdot(xb, w_in_ref[:, lo:lo + D_A]) + b_in_ref[:, lo:lo + D_A]

        u = sec(0) * jax.nn.sigmoid(sec(1))
        u_ext[pl.ds(HA + r0, CM), :] = u
        gb_s[pl.ds(r0, CM), :] = sec(2)
        v = sec(3) * sec(4)
        v_ext[pl.ds(HB + r0, CM), :] = v

    def conv_a(r0, anchors):
        zero = None
        n_group = 0
        for j in range(D_A // LANES):
            ls = slice(j * LANES, (j + 1) * LANES)
            wb = [jnp.broadcast_to(caw_ref[k:k + 1, ls], (SUBLANES, LANES)) for k in range(W_A)]
            for q in range(0, B, SUBLANES):
                for t0 in range(0, CM // B, G):
                    base = r0 + t0 * B + q
                    if anchors is not None:
                        a = anchors[min(n_group, len(anchors) - 1)]
                        zero = a if zero is None else zero + a
                    n_group += 1
                    wg = wb if zero is None else [w + zero for w in wb]
                    accs = [None] * G
                    for m in range(G + W_A - 1):
                        xin = u_ext[pl.ds(base + m * B, SUBLANES), ls]
                        for i in range(max(0, m - (W_A - 1)), min(G, m + 1)):
                            term = wg[m - i] * xin
                            accs[i] = term if accs[i] is None else accs[i] + term
                    for i in range(G):
                        ca_s[pl.ds(base + i * B, SUBLANES), ls] = accs[i]
                    zero = _ordered_zero(accs[G - 1])

    def gating(r0, slot):
        ca = ca_s[pl.ds(r0, CH), :] + cab_ref[...]
        y_a = jax.nn.silu(_layer_norm(ca, lag_ref[...], lab_ref[...]))
        y_s[slot, pl.ds(r0, CH), 0:D_A] = y_a.astype(BF16)
        cb = None
        for k in range(W_B):
            term = cbw_ref[k:k + 1, :] * v_ext[pl.ds(r0 + k * B, CH), :]
            cb = term if cb is None else cb + term
        y_b = gb_s[pl.ds(r0, CH), :] * cb
        y_s[slot, pl.ds(r0, CH), D_A:D_A + D_B] = y_b.astype(BF16)

    def vector_stage(r0, anchors, slot):
        conv_a(r0, anchors)
        for s in range(CM // CH):
            gating(r0 + s * CH, slot)

    def write_state():
        na_ref[...] = u_ext[R:R + HA, :]
        nb_ref[...] = v_ext[R:R + HB, :]
        if n_tiles > 1:
            u_ext[0:HA, :] = u_ext[R:R + HA, :]
            v_ext[0:HB, :] = v_ext[R:R + HB, :]

    _run_layer(in_proj, vector_stage, write_state, read_x, read_xp, y_s, _rows_writer(o_ref, False),
               (w_out_ref, b_out_ref, g1_ref, b1_ref), (w1_ref, w2_ref, g2_ref, b2_ref), R, CM, n_tiles)


def _odd_layer_kernel(x_ref, xp_ref, sc_ref, sh_ref, w_in_ref, b_in_ref, ccw_ref, ccb_ref,
                      wg_ref, bga_ref, bgx_ref, lam_ref, w_out_ref, b_out_ref, g1_ref, b1_ref,
                      w1_ref, w2_ref, g2_ref, b2_ref,
                      o_ref, nc_ref, nh_ref,
                      uc_ext, g_s, a_s, b_s, gg_s, h_s, y_s, *, B, R, CM, n_tiles, o_batch_major):
    HC = (W_C - 1) * B

    @pl.when(pl.program_id(0) == 0)
    def _():
        uc_ext[0:HC, :] = sc_ref[...]
        h_s[...] = sh_ref[...]

    read_x = _rows_reader(x_ref, False)

    def front(r0):
        rows = pl.ds(r0, CM)
        xb = read_x(r0, CM).astype(BF16)
        gate = _bdot(xb, w_in_ref[:, 0:D_RNN]) + b_in_ref[:, 0:D_RNN]
        gg_s[rows, :] = jax.nn.gelu(gate)
        u = _bdot(xb, w_in_ref[:, D_RNN:2 * D_RNN]) + b_in_ref[:, D_RNN:2 * D_RNN]
        uc_ext[pl.ds(HC + r0, CM), :] = u
        xc = None
        for k in range(W_C):
            term = ccw_ref[k:k + 1, :] * uc_ext[pl.ds(r0 + k * B, CM), :]
            xc = term if xc is None else xc + term
        xc = xc + ccb_ref[...]
        b_s[rows, :] = xc
        for h in range(LRU_HEADS):
            hs = slice(h * LRU_BW, (h + 1) * LRU_BW)
            g_s[rows, 2 * h * LRU_BW:2 * (h + 1) * LRU_BW] = _bdot(xc[:, hs].astype(BF16), wg_ref[h])

    lam = lam_ref[...]
    sp = jnp.maximum(-lam, 0.0) + jnp.log1p(jnp.exp(-jnp.abs(lam)))

    def gates(r0, anchors):
        rows = pl.ds(r0, CM)
        for h in range(LRU_HEADS):
            hs = slice(h * LRU_BW, (h + 1) * LRU_BW)
            g = g_s[rows, 2 * h * LRU_BW:2 * (h + 1) * LRU_BW]
            bga = bga_ref[:, hs]
            bgx = bgx_ref[:, hs]
            if anchors is not None:
                anchor = anchors[min(h, len(anchors) - 1)]
                bga = bga + anchor[0:1, :]
                bgx = bgx + anchor[0:1, :]
            r = jax.nn.sigmoid(g[:, 0:LRU_BW] + bga)
            i = jax.nn.sigmoid(g[:, LRU_BW:2 * LRU_BW] + bgx)
            log_a = -LRU_C * r * sp[:, hs]
            a = jnp.exp(log_a)
            mult = jnp.sqrt(jnp.maximum(1.0 - a * a, 0.0))
            a_s[rows, hs] = a
            b_s[rows, hs] = mult * (i * b_s[rows, hs])

    h_carry = [None]

    def vector_stage(r0, anchors, slot):
        gates(r0, anchors)
        h = h_s[...] if h_carry[0] is None else h_carry[0]
        for t in range(CM // B):
            rows = pl.ds(r0 + t * B, B)
            h = a_s[rows, :] * h + b_s[rows, :]
            b_s[rows, :] = h
        h_carry[0] = h
        rows = pl.ds(r0, CM)
        y_s[slot, rows, :] = (b_s[rows, :] * gg_s[rows, :]).astype(BF16)

    def write_state():
        nc_ref[...] = uc_ext[R:R + HC, :]
        nh_ref[...] = h_carry[0]
        if n_tiles > 1:
            h_s[...] = h_carry[0]
            uc_ext[0:HC, :] = uc_ext[R:R + HC, :]

    _run_layer(front, vector_stage, write_state, read_x, _rows_reader(xp_ref, False), y_s,
               _rows_writer(o_ref, o_batch_major),
               (w_out_ref, b_out_ref, g1_ref, b1_ref), (w1_ref, w2_ref, g2_ref, b2_ref), R, CM, n_tiles)


def _const_spec(shape):
    zeros = (0,) * len(shape)
    return pl.BlockSpec(shape, lambda i: zeros, pipeline_mode=pl.Buffered(1))


def _tile_spec(R, index, batch_major):
    if batch_major:
        return pl.BlockSpec((SUBLANES, R // SUBLANES, D_MODEL), lambda i: (0, index(i), 0))
    return pl.BlockSpec((R, D_MODEL), lambda i: (index(i), 0))


def _grid_and_row_specs(R, n_tiles, x_batch_major=False, o_batch_major=False):
    if n_tiles == 1:
        same = lambda i: i
        return ((1,), _tile_spec(R, same, x_batch_major), _tile_spec(R, same, x_batch_major),
                _tile_spec(R, same, o_batch_major))
    this = lambda i: jnp.minimum(i, n_tiles - 1)
    lag = lambda i: jnp.maximum(i - 1, 0)
    return ((n_tiles + 1,), _tile_spec(R, this, x_batch_major), _tile_spec(R, lag, x_batch_major),
            _tile_spec(R, lag, o_batch_major))


def _y_scratch(R, n_tiles):
    return pltpu.VMEM((2 if n_tiles > 1 else 1, R, D_MODEL), BF16)


def _tiling(n_rows, B):
    if n_rows <= SINGLE_TILE_MAX_ROWS:
        R = n_rows
    else:
        R = _largest_divisor(n_rows, ROW_TILE_TARGET, math.lcm(B, BF16_ROWS))
    n_tiles = n_rows // R
    CM = _largest_divisor(R, MATMUL_CHUNK_TARGET, math.lcm(B, BF16_ROWS))
    return R, CM, n_tiles


def _params():
    return pltpu.CompilerParams(dimension_semantics=("arbitrary",),
                                vmem_limit_bytes=VMEM_LIMIT_BYTES)


_MLP_KEYS = ('w_mlp1', 'w_mlp2', 'ln2_g', 'ln2_b')


def _even_layer(x, sa, sb, prm, B):
    x_batch_major = x.ndim == 3
    assert not x_batch_major or x.shape[0] == B == SUBLANES
    n_rows = x.shape[0] * x.shape[1] if x_batch_major else x.shape[0]
    R, CM, n_tiles = _tiling(n_rows, B)
    CH = _largest_divisor(CM, CONV_CHUNK, BF16_ROWS)
    G = _largest_divisor(CM // B, CONV_GROUP, 1)
    HA, HB = (W_A - 1) * B, (W_B - 1) * B
    assert n_tiles == 1 or R >= HA
    consts = [sa, sb] + [prm[k] for k in ('w_in', 'b_in', 'conv_a_w', 'conv_a_b', 'ln_a_g', 'ln_a_b',
                                          'conv_b_w', 'w_out', 'b_out', 'ln1_g', 'ln1_b') + _MLP_KEYS]
    kern = functools.partial(_even_layer_kernel, B=B, R=R, CM=CM, CH=CH, G=G, n_tiles=n_tiles,
                             x_batch_major=x_batch_major)
    grid, x_spec, xp_spec, o_spec = _grid_and_row_specs(R, n_tiles, x_batch_major=x_batch_major)
    return pl.pallas_call(
        kern,
        grid=grid,
        in_specs=[x_spec, xp_spec] + [_const_spec(c.shape) for c in consts],
        out_specs=[o_spec, _const_spec((HA, D_A)), _const_spec((HB, D_B))],
        out_shape=[jax.ShapeDtypeStruct((n_rows, D_MODEL), F32),
                   jax.ShapeDtypeStruct((HA, D_A), F32),
                   jax.ShapeDtypeStruct((HB, D_B), F32)],
        scratch_shapes=[pltpu.VMEM((HA + R, D_A), F32), pltpu.VMEM((HB + R, D_B), F32),
                        pltpu.VMEM((R, D_B), F32), pltpu.VMEM((R, D_A), F32), _y_scratch(R, n_tiles)],
        compiler_params=_params(),
        name="even_layer",
    )(x, x, *consts)


def _odd_layer(x, sc, sh, prm, B, o_batch_major=False):
    assert not o_batch_major or B == SUBLANES
    n_rows = x.shape[0]
    R, CM, n_tiles = _tiling(n_rows, B)
    HC = (W_C - 1) * B
    assert n_tiles == 1 or R >= HC
    consts = [sc, sh] + [prm[k] for k in ('w_in', 'b_in', 'conv_c_w', 'conv_c_b', 'w_gate', 'b_gate_a',
                                          'b_gate_x', 'lru_lambda', 'w_out', 'b_out', 'ln1_g',
                                          'ln1_b') + _MLP_KEYS]
    kern = functools.partial(_odd_layer_kernel, B=B, R=R, CM=CM, n_tiles=n_tiles, o_batch_major=o_batch_major)
    grid, x_spec, xp_spec, o_spec = _grid_and_row_specs(R, n_tiles, o_batch_major=o_batch_major)
    o_shape = (B, n_rows // B, D_MODEL) if o_batch_major else (n_rows, D_MODEL)
    rows_f32 = lambda width: pltpu.VMEM((R, width), F32)
    return pl.pallas_call(
        kern,
        grid=grid,
        in_specs=[x_spec, xp_spec] + [_const_spec(c.shape) for c in consts],
        out_specs=[o_spec, _const_spec((HC, D_RNN)), _const_spec((B, D_RNN))],
        out_shape=[jax.ShapeDtypeStruct(o_shape, F32),
                   jax.ShapeDtypeStruct((HC, D_RNN), F32),
                   jax.ShapeDtypeStruct((B, D_RNN), F32)],
        scratch_shapes=[pltpu.VMEM((HC + R, D_RNN), F32), rows_f32(2 * D_RNN),
                        rows_f32(D_RNN), rows_f32(D_RNN), rows_f32(D_RNN),
                        pltpu.VMEM((B, D_RNN), F32), _y_scratch(R, n_tiles)],
        compiler_params=_params(),
        name="odd_layer",
    )(x, x, *consts)


def _trunk(x, states, layers, B, batch_major_io=False):
    assert DEPTH % 2 == 0
    new_states = []
    for l in range(DEPTH):
        if l % 2 == 0:
            x, s0, s1 = _even_layer(x, states[l][0], states[l][1], layers[l], B)
        else:
            x, s0, s1 = _odd_layer(x, states[l][0], states[l][1], layers[l], B,
                                   o_batch_major=batch_major_io and l == DEPTH - 1)
        new_states.append((s0, s1))
    return x, new_states


def _to_rows(a, b_pad):
    bsz, t, c = a.shape
    a = jnp.transpose(a, (1, 0, 2))
    if b_pad > bsz:
        a = jnp.pad(a, ((0, 0), (0, b_pad - bsz), (0, 0)))
    return a.reshape(t * b_pad, c)


def _from_rows(rows, b_pad, lo, hi):
    c = rows.shape[-1]
    a = rows.reshape(-1, b_pad, c)[:, lo:hi]
    return jnp.transpose(a, (1, 0, 2))


def kernel(x_prompt, x_sample, state_conv_a, state_conv_b, state_conv_c, state_lru, meta_tokens,
           ln1_g, ln1_b, ln2_g, ln2_b,
           w_in_e, b_in_e, conv_a_w, conv_a_b, ln_a_g, ln_a_b, conv_b_w, w_out_e, b_out_e,
           w_in_o, b_in_o, conv_c_w, conv_c_b, w_gate_a, b_gate_a, w_gate_x, b_gate_x, lru_lambda,
           w_out_o, b_out_o, w_mlp1, w_mlp2):
    bsz, seq, _ = x_prompt.shape
    dec_b, dec_t, _ = x_sample.shape
    n_meta = meta_tokens.shape[0]
    assert dec_t == n_meta, "meta tokens ride along with the sample streams as one more sequence"
    assert bsz == SUBLANES, "the prompt streams fill exactly one sublane group"

    row = lambda v: v.reshape(1, -1)
    layers = []
    for l in range(DEPTH):
        i = l // 2
        prm = dict(ln1_g=row(ln1_g[l]), ln1_b=row(ln1_b[l]), ln2_g=row(ln2_g[l]), ln2_b=row(ln2_b[l]),
                   w_mlp1=w_mlp1[l].astype(BF16), w_mlp2=w_mlp2[l].astype(BF16))
        if l % 2 == 0:
            prm.update(w_in=w_in_e[i].astype(BF16), b_in=row(b_in_e[i]), conv_a_w=conv_a_w[i],
                       conv_a_b=row(conv_a_b[i]), ln_a_g=row(ln_a_g[i]), ln_a_b=row(ln_a_b[i]),
                       conv_b_w=conv_b_w[i], w_out=w_out_e[i].astype(BF16), b_out=row(b_out_e[i]))
        else:
            prm.update(w_in=w_in_o[i].astype(BF16), b_in=row(b_in_o[i]), conv_c_w=conv_c_w[i],
                       conv_c_b=row(conv_c_b[i]),
                       w_gate=jnp.concatenate([w_gate_a[i], w_gate_x[i]], axis=-1).astype(BF16),
                       b_gate_a=row(b_gate_a[i]), b_gate_x=row(b_gate_x[i]),
                       lru_lambda=row(lru_lambda[i]), w_out=w_out_o[i].astype(BF16),
                       b_out=row(b_out_o[i]))
        layers.append(prm)

    n_short = dec_b + 1
    bp = -(-n_short // SUBLANES) * SUBLANES
    xs = jnp.concatenate([meta_tokens[None].astype(x_sample.dtype), x_sample], axis=0)
    with_zero = lambda s: jnp.concatenate([jnp.zeros_like(s[:1]), s], axis=0)
    states = []
    for l in range(DEPTH):
        i = l // 2
        if l % 2 == 0:
            states.append((_to_rows(with_zero(state_conv_a[i]), bp), _to_rows(with_zero(state_conv_b[i]), bp)))
        else:
            states.append((_to_rows(with_zero(state_conv_c[i]), bp),
                           _to_rows(with_zero(state_lru[i])[:, None, :], bp)))
    ys_rows, new_short = _trunk(_to_rows(xs, bp), states, layers, bp)

    def meta_state(rows):
        c = rows.shape[-1]
        m = rows.reshape(-1, bp, c)[:, 0:1]
        return jnp.broadcast_to(m, (m.shape[0], bsz, c)).reshape(-1, c)

    p_states = [(meta_state(s0), meta_state(s1)) for (s0, s1) in new_short]
    y_prompt, new_prompt = _trunk(x_prompt, p_states, layers, bsz, batch_major_io=True)

    y_sample = _from_rows(ys_rows, bp, 1, n_short)

    def collect(new, b_pad, lo, hi):
        sa = jnp.stack([_from_rows(new[l][0], b_pad, lo, hi) for l in range(0, DEPTH, 2)])
        sb = jnp.stack([_from_rows(new[l][1], b_pad, lo, hi) for l in range(0, DEPTH, 2)])
        sc = jnp.stack([_from_rows(new[l][0], b_pad, lo, hi) for l in range(1, DEPTH, 2)])
        sh = jnp.stack([_from_rows(new[l][1], b_pad, lo, hi)[:, 0] for l in range(1, DEPTH, 2)])
        return sa, sb, sc, sh

    sa_p, sb_p, sc_p, sh_p = collect(new_prompt, bsz, 0, bsz)
    sa_s, sb_s, sc_s, sh_s = collect(new_short, bp, 1, n_short)
    return (y_prompt, y_sample, sa_p, sb_p, sc_p, sh_p, sa_s, sb_s, sc_s, sh_s)
```
